```python
import jax
import jax.numpy as jnp
from jax import lax
import numpy as np

D_MODEL = 4096
BATCH = 2
SEQ = 4096
DEPTH = 1
DEC_BATCH = 128
DEC_SEQ = 4
PAST_LEN = 2048
PAGE_SIZE = 128

N_HEADS_NSA = 16
N_KV_NSA = 4
HPG = N_HEADS_NSA // N_KV_NSA
HEAD_DIM = 128
CMP_LEN = 32
CMP_STRIDE = 16
CMP_HIDDEN = 256
SEL_BLOCK = 64
SEL_TOP = 16
WINDOW = 512
Q_BLOCK = 128
D_NSA = N_HEADS_NSA * HEAD_DIM
D_KV = N_KV_NSA * HEAD_DIM
SCALE = HEAD_DIM ** -0.5
N_HEADS_RET = 8
RET_DK = 256
RET_DV = 256
RET_CHUNK = 128
D_RET = N_HEADS_RET * RET_DV
ROPE_BASE = 10000.0
D_FF = 11008
EPS = 1e-6

kernel_name = 'nsa_retention_macaron_step'


def rmsnorm(x, w):
    xf = x.astype(jnp.float32)
    xf = xf * lax.rsqrt(jnp.mean(xf * xf, axis=-1, keepdims=True) + EPS)
    return (xf * w.astype(jnp.float32)).astype(x.dtype)


def masked_softmax(s, mask):
    s = jnp.where(mask, s, -1e30)
    m = jnp.max(s, axis=-1, keepdims=True)
    e = jnp.exp(s - m) * mask
    return e / jnp.maximum(jnp.sum(e, axis=-1, keepdims=True), 1e-30)


def alibi_slopes():
    h = jnp.arange(1, N_HEADS_NSA + 1, dtype=jnp.float32)
    return (2.0 ** (-8.0 * h / N_HEADS_NSA)).reshape(N_KV_NSA, HPG)


def ffn_half(x, g, w_gate, w_up, w_down):
    h = rmsnorm(x, g)
    return x + 0.5 * ((jax.nn.silu(h @ w_gate) * (h @ w_up)) @ w_down)


def mixer_projections(x, norm_mix, w_in):
    B, T, _ = x.shape
    z = rmsnorm(x, norm_mix) @ w_in
    sizes = [D_NSA] + [D_KV] * 6 + [3 * N_HEADS_NSA, N_HEADS_RET * RET_DK, N_HEADS_RET * RET_DK, D_RET, D_RET, D_MODEL, D_MODEL]
    parts = jnp.split(z, [int(c) for c in np.cumsum(sizes)[:-1]], axis=-1)
    q = parts[0].reshape(B, T, N_KV_NSA, HPG, HEAD_DIM)
    kvs = [p.reshape(B, T, N_KV_NSA, HEAD_DIM) for p in parts[1:7]]
    return (q, *kvs, *parts[7:])


def compress_blocks(k, pos_emb, w1, w2):
    B, L, G, D = k.shape
    lp = -(-L // CMP_STRIDE) * CMP_STRIDE
    sub = jnp.pad(k, ((0, 0), (0, lp - L), (0, 0), (0, 0))).reshape(B, lp // CMP_STRIDE, CMP_STRIDE, G, D)
    r = CMP_LEN // CMP_STRIDE
    nc = lp // CMP_STRIDE - r + 1
    blk = jnp.concatenate([sub[:, j:j + nc] for j in range(r)], axis=2)
    blk = blk + pos_emb[None, None, :, None, :]
    blk = jnp.transpose(blk, (0, 1, 3, 2, 4)).reshape(B, nc, G, CMP_LEN * D)
    return jax.nn.gelu(blk @ w1) @ w2


def nsa_compressed_selected(qg, q_pos, k_cmp, v_cmp, k_sel, v_sel, cmp_pos_k, cmp_pos_v, w_cmp_k1, w_cmp_k2, w_cmp_v1, w_cmp_v2):
    B, T = qg.shape[:2]
    L = k_cmp.shape[1]
    f32 = jnp.float32
    slopes = alibi_slopes()
    kc = compress_blocks(k_cmp, cmp_pos_k, w_cmp_k1, w_cmp_k2)
    vc = compress_blocks(v_cmp, cmp_pos_v, w_cmp_v1, w_cmp_v2)
    nc = kc.shape[1]
    c_start = jnp.arange(nc) * CMP_STRIDE
    dist_c = q_pos[:, None] - (c_start + CMP_LEN - 1)[None, :]
    s = jnp.einsum('btgrd,bngd->bgrtn', qg, kc).astype(f32) * SCALE - slopes[None, :, :, None, None] * dist_c.astype(f32)
    p_cmp = masked_softmax(s, dist_c >= 0)
    o_cmp = jnp.einsum('bgrtn,bngd->btgrd', p_cmp.astype(vc.dtype), vc)
    ns = -(-L // SEL_BLOCK)
    s_start = jnp.arange(ns) * SEL_BLOCK
    overlap = ((c_start[:, None] < s_start[None, :] + SEL_BLOCK) & (c_start[:, None] + CMP_LEN > s_start[None, :])).astype(f32)
    imp = jnp.einsum('bgrtn,ns->bgts', p_cmp, overlap)
    t_blk = q_pos // SEL_BLOCK
    blk = jnp.arange(ns)[None, :]
    forced = (blk == 0) | (blk == t_blk[:, None]) | (blk == t_blk[:, None] - 1)
    score = jnp.where(forced, 1e9, imp)
    score = jnp.where(s_start[None, :] <= q_pos[:, None], score, -1e9)
    n_top = min(SEL_TOP, ns)
    _, idx = lax.top_k(score, n_top)
    def to_blocks(a):
        a = jnp.pad(a, ((0, 0), (0, ns * SEL_BLOCK - L), (0, 0), (0, 0)))
        return a.reshape(B, ns, SEL_BLOCK, N_KV_NSA, HEAD_DIM).transpose(0, 3, 1, 2, 4)
    ks, vs = to_blocks(k_sel), to_blocks(v_sel)
    tb = Q_BLOCK if T % Q_BLOCK == 0 else T
    nb = T // tb
    q_blocks = jnp.moveaxis(qg.reshape(B, nb, tb, N_KV_NSA, HPG, HEAD_DIM), 1, 0)
    i_blocks = jnp.moveaxis(idx.reshape(B, N_KV_NSA, nb, tb, n_top), 2, 0)
    p_blocks = q_pos.reshape(nb, tb)
    gather = jax.vmap(jax.vmap(lambda kb, ix: kb[ix]))
    nk = n_top * SEL_BLOCK

    def one_block(args):
        qb, ib, pb = args
        kg = gather(ks, ib).reshape(B, N_KV_NSA, tb, nk, HEAD_DIM)
        vg = gather(vs, ib).reshape(B, N_KV_NSA, tb, nk, HEAD_DIM)
        kpos = (ib[..., None] * SEL_BLOCK + jnp.arange(SEL_BLOCK)).reshape(B, N_KV_NSA, tb, nk)
        dist = pb[None, None, :, None] - kpos
        sb = jnp.einsum('btgrd,bgtkd->bgrtk', qb, kg).astype(f32) * SCALE - slopes[None, :, :, None, None] * dist[:, :, None].astype(f32)
        pb_ = masked_softmax(sb, (dist >= 0)[:, :, None])
        return jnp.einsum('bgrtk,bgtkd->btgrd', pb_.astype(vg.dtype), vg)

    o_sel = lax.map(one_block, (q_blocks, i_blocks, p_blocks))
    o_sel = jnp.moveaxis(o_sel, 0, 1).reshape(B, T, N_KV_NSA, HPG, HEAD_DIM)
    return o_cmp, o_sel


def window_banded(qg, kw, vw):
    B, T = qg.shape[:2]
    nb, nw = T // Q_BLOCK, WINDOW // Q_BLOCK

    def bands(a):
        ap = jnp.pad(a, ((0, 0), (WINDOW, 0), (0, 0), (0, 0))).reshape(B, nb + nw, Q_BLOCK, N_KV_NSA, HEAD_DIM)
        return jnp.concatenate([ap[:, j:j + nb] for j in range(nw + 1)], axis=2)

    kb, vb = bands(kw), bands(vw)
    qb = qg.reshape(B, nb, Q_BLOCK, N_KV_NSA, HPG, HEAD_DIM)
    qpos = jnp.arange(T).reshape(nb, Q_BLOCK)
    kpos = jnp.arange(nb)[:, None] * Q_BLOCK - WINDOW + jnp.arange(WINDOW + Q_BLOCK)[None, :]
    dist = qpos[:, :, None] - kpos[:, None, :]
    mask = (dist >= 0) & (dist <= WINDOW) & (kpos[:, None, :] >= 0)
    slopes = alibi_slopes()
    s = jnp.einsum('bnqgrd,bnkgd->bngrqk', qb, kb).astype(jnp.float32) * SCALE - slopes[None, None, :, :, None, None] * dist[None, :, None, None].astype(jnp.float32)
    p = masked_softmax(s, mask[None, :, None, None])
    o = jnp.einsum('bngrqk,bnkgd->bnqgrd', p.astype(vb.dtype), vb)
    return o.reshape(B, T, N_KV_NSA, HPG, HEAD_DIM)


def window_buffer(qg, pos, kw, vw, buf_k, buf_v):
    wb, T = buf_k.shape[1], kw.shape[1]
    kk = jnp.concatenate([buf_k, kw], axis=1)
    vv = jnp.concatenate([buf_v, vw], axis=1)
    kpos = PAST_LEN - wb + jnp.arange(wb + T)
    dist = pos[:, None] - kpos[None, :]
    mask = (dist >= 0) & (dist <= WINDOW)
    slopes = alibi_slopes()
    s = jnp.einsum('btgrd,bkgd->bgrtk', qg, kk).astype(jnp.float32) * SCALE - slopes[None, :, :, None, None] * dist.astype(jnp.float32)
    p = masked_softmax(s, mask)
    o = jnp.einsum('bgrtk,bkgd->btgrd', p.astype(vv.dtype), vv)
    keep = min(WINDOW, wb + T)
    return o, kk[:, wb + T - keep:], vv[:, wb + T - keep:]


def nsa_combine(o_cmp, o_sel, o_win, gate_nsa):
    B, T = gate_nsa.shape[:2]
    g = jax.nn.sigmoid(gate_nsa).reshape(B, T, 3, N_KV_NSA, HPG, 1)
    o = g[:, :, 0] * o_cmp + g[:, :, 1] * o_sel + g[:, :, 2] * o_win
    return o.reshape(B, T, D_NSA)


def rotary(x, pos):
    half = x.shape[-1] // 2
    inv = ROPE_BASE ** (-jnp.arange(half, dtype=jnp.float32) / half)
    ang = pos.astype(jnp.float32)[:, None] * inv[None, :]
    c, s = jnp.cos(ang)[None, :, None, :], jnp.sin(ang)[None, :, None, :]
    x1, x2 = x[..., :half], x[..., half:]
    return jnp.concatenate([x1 * c - x2 * s, x1 * s + x2 * c], axis=-1)


def retention(qr, kr, vr, gr, pos, s0, chunk, ret_gn):
    B, T, _ = qr.shape
    f32 = jnp.float32
    q = rotary(qr.reshape(B, T, N_HEADS_RET, RET_DK).astype(f32), pos)
    k = rotary(kr.reshape(B, T, N_HEADS_RET, RET_DK).astype(f32), pos) * (RET_DK ** -0.5)
    v = vr.reshape(B, T, N_HEADS_RET, RET_DV).astype(f32)
    lg = jnp.log1p(-(2.0 ** (-5.0 - jnp.arange(N_HEADS_RET, dtype=f32))))
    i = jnp.arange(chunk)
    diff = i[:, None] - i[None, :]
    intra = jnp.where(diff >= 0, jnp.exp(lg[:, None, None] * jnp.maximum(diff, 0)), 0.0)
    q_dec = jnp.exp(lg[None, :] * (i[:, None] + 1))
    k_dec = jnp.exp(lg[None, :] * (chunk - 1 - i)[:, None])
    c_dec = jnp.exp(lg * chunk)
    nc = T // chunk

    def to_chunks(a):
        return jnp.moveaxis(a.reshape(B, nc, chunk, N_HEADS_RET, a.shape[-1]), 1, 0)

    def step(S, inp):
        qc, kc, vc = inp
        a = jnp.einsum('bihd,bjhd->bhij', qc, kc) * intra[None]
        o = jnp.einsum('bhij,bjhe->bihe', a, vc) + jnp.einsum('bihd,bhde->bihe', qc, S) * q_dec[None, :, :, None]
        S = S * c_dec[None, :, None, None] + jnp.einsum('bjhd,jh,bjhe->bhde', kc, k_dec, vc)
        return S, o

    S, o = lax.scan(step, s0.astype(f32), (to_chunks(q), to_chunks(k), to_chunks(v)))
    o = jnp.moveaxis(o, 0, 1).reshape(B, T, N_HEADS_RET, RET_DV)
    mu = jnp.mean(o, axis=-1, keepdims=True)
    var = jnp.mean(jnp.square(o - mu), axis=-1, keepdims=True)
    on = ((o - mu) * lax.rsqrt(var + EPS)).reshape(B, T, D_RET) * ret_gn.astype(f32)
    return jax.nn.silu(gr) * on.astype(gr.dtype), S


def merge_out(x, o_nsa, o_ret, ga, gb, w_br_nsa, w_br_ret, w_out):
    y = jax.nn.sigmoid(ga) * (o_nsa @ w_br_nsa) + jax.nn.sigmoid(gb) * (o_ret @ w_br_ret)
    return x + y @ w_out


def gather_pages(pool, page_table):
    g = pool[page_table]
    return g.reshape(g.shape[0], g.shape[1] * g.shape[2], g.shape[3], g.shape[4])


def prompt_layer(x, W):
    (norm_ffn1, w_ffn1_gate, w_ffn1_up, w_ffn1_down, norm_mix, w_in, cmp_pos_k, cmp_pos_v, w_cmp_k1, w_cmp_k2, w_cmp_v1, w_cmp_v2,
     ret_gn, w_br_nsa, w_br_ret, w_out, norm_ffn2, w_ffn2_gate, w_ffn2_up, w_ffn2_down) = W
    B, T, _ = x.shape
    pos = jnp.arange(T)
    x = ffn_half(x, norm_ffn1, w_ffn1_gate, w_ffn1_up, w_ffn1_down)
    q, kc, vc, ks, vs, kw, vw, gn, qr, kr, vr, gr, ga, gb = mixer_projections(x, norm_mix, w_in)
    o_cmp, o_sel = nsa_compressed_selected(q, pos, kc, vc, ks, vs, cmp_pos_k, cmp_pos_v, w_cmp_k1, w_cmp_k2, w_cmp_v1, w_cmp_v2)
    o_win = window_banded(q, kw, vw)
    o_nsa = nsa_combine(o_cmp, o_sel, o_win, gn)
    s0 = jnp.zeros((B, N_HEADS_RET, RET_DK, RET_DV), jnp.float32)
    o_ret, S = retention(qr, kr, vr, gr, pos, s0, RET_CHUNK if T % RET_CHUNK == 0 else T, ret_gn)
    x = merge_out(x, o_nsa, o_ret, ga, gb, w_br_nsa, w_br_ret, w_out)
    x = ffn_half(x, norm_ffn2, w_ffn2_gate, w_ffn2_up, w_ffn2_down)
    wl = min(WINDOW, T)
    return x, (kc, vc, ks, vs, kw[:, T - wl:], vw[:, T - wl:], S)


def sample_layer(x, c_cmp_k, c_cmp_v, c_sel_k, c_sel_v, c_win_k, c_win_v, s_ret, page_table, W):
    (norm_ffn1, w_ffn1_gate, w_ffn1_up, w_ffn1_down, norm_mix, w_in, cmp_pos_k, cmp_pos_v, w_cmp_k1, w_cmp_k2, w_cmp_v1, w_cmp_v2,
     ret_gn, w_br_nsa, w_br_ret, w_out, norm_ffn2, w_ffn2_gate, w_ffn2_up, w_ffn2_down) = W
    B, T, _ = x.shape
    pos = PAST_LEN + jnp.arange(T)
    x = ffn_half(x, norm_ffn1, w_ffn1_gate, w_ffn1_up, w_ffn1_down)
    q, kc, vc, ks, vs, kw, vw, gn, qr, kr, vr, gr, ga, gb = mixer_projections(x, norm_mix, w_in)
    kc_all = jnp.concatenate([gather_pages(c_cmp_k, page_table), kc], axis=1)
    vc_all = jnp.concatenate([gather_pages(c_cmp_v, page_table), vc], axis=1)
    ks_all = jnp.concatenate([gather_pages(c_sel_k, page_table), ks], axis=1)
    vs_all = jnp.concatenate([gather_pages(c_sel_v, page_table), vs], axis=1)
    o_cmp, o_sel = nsa_compressed_selected(q, pos, kc_all, vc_all, ks_all, vs_all, cmp_pos_k, cmp_pos_v, w_cmp_k1, w_cmp_k2, w_cmp_v1, w_cmp_v2)
    o_win, kw_buf, vw_buf = window_buffer(q, pos, kw, vw, c_win_k, c_win_v)
    o_nsa = nsa_combine(o_cmp, o_sel, o_win, gn)
    o_ret, S = retention(qr, kr, vr, gr, pos, s_ret, T, ret_gn)
    x = merge_out(x, o_nsa, o_ret, ga, gb, w_br_nsa, w_br_ret, w_out)
    x = ffn_half(x, norm_ffn2, w_ffn2_gate, w_ffn2_up, w_ffn2_down)
    return x, (kc, vc, ks, vs, kw_buf, vw_buf, S)


def setup_inputs(seed: int = 0) -> dict:
    key = jax.random.key(seed)
    k = jax.random.split(key, 32)
    f32 = jnp.float32
    n_pages = PAST_LEN // PAGE_SIZE
    n_pool = (DEC_BATCH * n_pages * 5) // 4
    win_len = min(WINDOW, PAST_LEN)
    d_in = D_NSA + 6 * D_KV + 3 * N_HEADS_NSA + 2 * N_HEADS_RET * RET_DK + 2 * D_RET + 2 * D_MODEL

    def nrm(kk, shape, scale=1.0):
        return jax.random.normal(kk, shape, f32) * scale

    def gain(kk, n):
        return 1.0 + 0.01 * jax.random.normal(kk, (DEPTH, n), f32)

    kv_pool = (DEPTH, n_pool, PAGE_SIZE, N_KV_NSA, HEAD_DIM)
    win = (DEPTH, DEC_BATCH, win_len, N_KV_NSA, HEAD_DIM)
    page_table = jax.random.permutation(k[9], n_pool)[:DEC_BATCH * n_pages].reshape(DEC_BATCH, n_pages).astype(jnp.int32)
    return {
        'x_prompt': nrm(k[0], (BATCH, SEQ, D_MODEL)),
        'x_sample': nrm(k[1], (DEC_BATCH, DEC_SEQ, D_MODEL)),
        'cache_cmp_k': nrm(k[2], kv_pool),
        'cache_cmp_v': nrm(k[3], kv_pool),
        'cache_sel_k': nrm(k[4], kv_pool),
        'cache_sel_v': nrm(k[5], kv_pool),
        'cache_win_k': nrm(k[6], win),
        'cache_win_v': nrm(k[7], win),
        'state_ret': nrm(k[8], (DEPTH, DEC_BATCH, N_HEADS_RET, RET_DK, RET_DV), 0.1),
        'page_table': page_table,
        'norm_ffn1': gain(k[10], D_MODEL),
        'w_ffn1_gate': nrm(k[11], (DEPTH, D_MODEL, D_FF), D_MODEL ** -0.5),
        'w_ffn1_up': nrm(k[12], (DEPTH, D_MODEL, D_FF), D_MODEL ** -0.5),
        'w_ffn1_down': nrm(k[13], (DEPTH, D_FF, D_MODEL), D_FF ** -0.5),
        'norm_mix': gain(k[14], D_MODEL),
        'w_in': nrm(k[15], (DEPTH, D_MODEL, d_in), D_MODEL ** -0.5),
        'cmp_pos_k': nrm(k[16], (DEPTH, CMP_LEN, HEAD_DIM), 0.02),
        'cmp_pos_v': nrm(k[17], (DEPTH, CMP_LEN, HEAD_DIM), 0.02),
        'w_cmp_k1': nrm(k[18], (DEPTH, CMP_LEN * HEAD_DIM, CMP_HIDDEN), (CMP_LEN * HEAD_DIM) ** -0.5),
        'w_cmp_k2': nrm(k[19], (DEPTH, CMP_HIDDEN, HEAD_DIM), CMP_HIDDEN ** -0.5),
        'w_cmp_v1': nrm(k[20], (DEPTH, CMP_LEN * HEAD_DIM, CMP_HIDDEN), (CMP_LEN * HEAD_DIM) ** -0.5),
        'w_cmp_v2': nrm(k[21], (DEPTH, CMP_HIDDEN, HEAD_DIM), CMP_HIDDEN ** -0.5),
        'ret_gn': gain(k[22], D_RET),
        'w_br_nsa': nrm(k[23], (DEPTH, D_NSA, D_MODEL), D_NSA ** -0.5),
        'w_br_ret': nrm(k[24], (DEPTH, D_RET, D_MODEL), D_RET ** -0.5),
        'w_out': nrm(k[25], (DEPTH, D_MODEL, D_MODEL), D_MODEL ** -0.5),
        'norm_ffn2': gain(k[26], D_MODEL),
        'w_ffn2_gate': nrm(k[27], (DEPTH, D_MODEL, D_FF), D_MODEL ** -0.5),
        'w_ffn2_up': nrm(k[28], (DEPTH, D_MODEL, D_FF), D_MODEL ** -0.5),
        'w_ffn2_down': nrm(k[29], (DEPTH, D_FF, D_MODEL), D_FF ** -0.5),
        'norm_final': 1.0 + 0.01 * jax.random.normal(k[30], (D_MODEL,), f32),
    }


def reference(x_prompt, x_sample, cache_cmp_k, cache_cmp_v, cache_sel_k, cache_sel_v, cache_win_k, cache_win_v, state_ret, page_table,
              norm_ffn1, w_ffn1_gate, w_ffn1_up, w_ffn1_down, norm_mix, w_in, cmp_pos_k, cmp_pos_v, w_cmp_k1, w_cmp_k2, w_cmp_v1, w_cmp_v2,
              ret_gn, w_br_nsa, w_br_ret, w_out, norm_ffn2, w_ffn2_gate, w_ffn2_up, w_ffn2_down, norm_final):
    yp, ys = x_prompt, x_sample
    p_states, s_states = [], []
    for l in range(DEPTH):
        W = (norm_ffn1[l], w_ffn1_gate[l], w_ffn1_up[l], w_ffn1_down[l], norm_mix[l], w_in[l], cmp_pos_k[l], cmp_pos_v[l],
             w_cmp_k1[l], w_cmp_k2[l], w_cmp_v1[l], w_cmp_v2[l], ret_gn[l], w_br_nsa[l], w_br_ret[l], w_out[l],
             norm_ffn2[l], w_ffn2_gate[l], w_ffn2_up[l], w_ffn2_down[l])
        yp, st_p = prompt_layer(yp, W)
        p_states.append(st_p)
        ys, st_s = sample_layer(ys, cache_cmp_k[l], cache_cmp_v[l], cache_sel_k[l], cache_sel_v[l], cache_win_k[l], cache_win_v[l],
                                state_ret[l], page_table, W)
        s_states.append(st_s)
    y_prompt = rmsnorm(yp, norm_final)
    y_sample = rmsnorm(ys, norm_final)
    p_cmp_k, p_cmp_v, p_sel_k, p_sel_v, p_win_k, p_win_v, p_ret = [jnp.stack(a) for a in zip(*p_states)]
    s_cmp_k, s_cmp_v, s_sel_k, s_sel_v, s_win_k, s_win_v, s_ret = [jnp.stack(a) for a in zip(*s_states)]
    return (y_prompt, y_sample, p_cmp_k, p_cmp_v, p_sel_k, p_sel_v, p_win_k, p_win_v, p_ret,
            s_cmp_k, s_cmp_v, s_sel_k, s_sel_v, s_win_k, s_win_v, s_ret)
```

```python
import functools

import jax
import jax.numpy as jnp
import numpy as np
from jax import lax
from jax.experimental import pallas as pl
from jax.experimental.pallas import tpu as pltpu

D_MODEL = 4096
DEPTH = 1
PAST_LEN = 2048
PAGE_SIZE = 128
N_HEADS_NSA = 16
N_KV_NSA = 4
HPG = N_HEADS_NSA // N_KV_NSA
HEAD_DIM = 128
CMP_LEN = 32
CMP_STRIDE = 16
CMP_HIDDEN = 256
SEL_BLOCK = 64
SEL_TOP = 16
WINDOW = 512
Q_BLOCK = 128
D_NSA = N_HEADS_NSA * HEAD_DIM
D_KV = N_KV_NSA * HEAD_DIM
SCALE = HEAD_DIM ** -0.5
N_HEADS_RET = 8
RET_DK = 256
RET_DV = 256
RET_CHUNK = 128
D_RET = N_HEADS_RET * RET_DV
ROPE_BASE = 10000.0
D_FF = 11008
EPS = 1e-6

VMEM_LIMIT_BYTES = 56 * 1024 * 1024
ROW_TILE = 512
FF_TILE = 256
COL_TILE = 512

bf16 = jnp.bfloat16
f32 = jnp.float32


def _rms_scale(x, gamma):
    ms = jnp.mean(x * x, axis=-1, keepdims=True)
    return x * lax.rsqrt(ms + EPS) * gamma


def _sigmoid(x):
    return 1.0 / (1.0 + jnp.exp(-x))


def _ffn_body(x_ref, g_ref, wg_ref, wu_ref, wd_ref, gf_ref, o_ref, h_ref, *, final_norm):
    j = pl.program_id(1)

    @pl.when(j == 0)
    def _():
        x = x_ref[...]
        h_ref[...] = _rms_scale(x, g_ref[...]).astype(bf16)
        o_ref[...] = x

    h = h_ref[...]
    gate = jnp.dot(h, wg_ref[...], preferred_element_type=f32)
    up = jnp.dot(h, wu_ref[...], preferred_element_type=f32)
    act = (gate * _sigmoid(gate) * up * 0.5).astype(bf16)
    o_ref[...] += jnp.dot(act, wd_ref[...], preferred_element_type=f32)

    if final_norm:
        @pl.when(j == pl.num_programs(1) - 1)
        def _():
            o_ref[...] = _rms_scale(o_ref[...], gf_ref[...])


def _ffn(x, gamma, wg, wu, wd, gamma_final=None):
    m, d = x.shape
    ff = wg.shape[1]
    final_norm = gamma_final is not None
    if gamma_final is None:
        gamma_final = gamma
    grid = (m // ROW_TILE, ff // FF_TILE)
    one = pl.Buffered(1)
    return pl.pallas_call(
        functools.partial(_ffn_body, final_norm=final_norm),
        grid=grid,
        in_specs=[
            pl.BlockSpec((ROW_TILE, d), lambda i, j: (i, 0), pipeline_mode=one),
            pl.BlockSpec((1, d), lambda i, j: (0, 0)),
            pl.BlockSpec((d, FF_TILE), lambda i, j: (0, j)),
            pl.BlockSpec((d, FF_TILE), lambda i, j: (0, j)),
            pl.BlockSpec((FF_TILE, d), lambda i, j: (j, 0)),
            pl.BlockSpec((1, d), lambda i, j: (0, 0)),
        ],
        out_specs=pl.BlockSpec((ROW_TILE, d), lambda i, j: (i, 0), pipeline_mode=one),
        out_shape=jax.ShapeDtypeStruct((m, d), f32),
        scratch_shapes=[pltpu.VMEM((ROW_TILE, d), bf16)],
        compiler_params=pltpu.CompilerParams(
            dimension_semantics=("arbitrary", "arbitrary"), vmem_limit_bytes=VMEM_LIMIT_BYTES),
        name="ffn",
    )(x, gamma.reshape(1, d), wg, wu, wd, gamma_final.reshape(1, d))


def _norm_matmul_body(x_ref, g_ref, w_ref, o_ref, h_ref):
    @pl.when(pl.program_id(1) == 0)
    def _():
        h_ref[...] = _rms_scale(x_ref[...], g_ref[...]).astype(bf16)

    o_ref[...] = jnp.dot(h_ref[...], w_ref[...], preferred_element_type=f32)


def _norm_matmul(x, gamma, w):
    m, d = x.shape
    n = w.shape[1]
    return pl.pallas_call(
        _norm_matmul_body,
        grid=(m // ROW_TILE, n // COL_TILE),
        in_specs=[
            pl.BlockSpec((ROW_TILE, d), lambda i, j: (i, 0)),
            pl.BlockSpec((1, d), lambda i, j: (0, 0)),
            pl.BlockSpec((d, COL_TILE), lambda i, j: (0, j)),
        ],
        out_specs=pl.BlockSpec((ROW_TILE, COL_TILE), lambda i, j: (i, j)),
        out_shape=jax.ShapeDtypeStruct((m, n), f32),
        scratch_shapes=[pltpu.VMEM((ROW_TILE, d), bf16)],
        compiler_params=pltpu.CompilerParams(
            dimension_semantics=("arbitrary", "arbitrary"), vmem_limit_bytes=VMEM_LIMIT_BYTES),
        name="norm_matmul",
    )(x, gamma.reshape(1, d), w)


def _gated_pair_body(a_ref, b_ref, ga_ref, gb_ref, wa_ref, wb_ref, o_ref):
    ya = jnp.dot(a_ref[...], wa_ref[...], preferred_element_type=f32)
    yb = jnp.dot(b_ref[...], wb_ref[...], preferred_element_type=f32)
    o_ref[...] = (_sigmoid(ga_ref[...]) * ya + _sigmoid(gb_ref[...]) * yb).astype(o_ref.dtype)


def _gated_pair_matmul(a, b, ga, gb, wa, wb):
    m, ka = a.shape
    kb = b.shape[1]
    n = wa.shape[1]
    return pl.pallas_call(
        _gated_pair_body,
        grid=(m // ROW_TILE, n // COL_TILE),
        in_specs=[
            pl.BlockSpec((ROW_TILE, ka), lambda i, j: (i, 0)),
            pl.BlockSpec((ROW_TILE, kb), lambda i, j: (i, 0)),
            pl.BlockSpec((ROW_TILE, COL_TILE), lambda i, j: (i, j)),
            pl.BlockSpec((ROW_TILE, COL_TILE), lambda i, j: (i, j)),
            pl.BlockSpec((ka, COL_TILE), lambda i, j: (0, j)),
            pl.BlockSpec((kb, COL_TILE), lambda i, j: (0, j)),
        ],
        out_specs=pl.BlockSpec((ROW_TILE, COL_TILE), lambda i, j: (i, j)),
        out_shape=jax.ShapeDtypeStruct((m, n), bf16),
        compiler_params=pltpu.CompilerParams(
            dimension_semantics=("arbitrary", "arbitrary"), vmem_limit_bytes=VMEM_LIMIT_BYTES),
        name="gated_pair_matmul",
    )(a, b, ga, gb, wa, wb)


def _residual_matmul_body(y_ref, w_ref, x_ref, o_ref):
    o_ref[...] = x_ref[...] + jnp.dot(y_ref[...], w_ref[...], preferred_element_type=f32)


def _residual_matmul(y, w, x):
    m, k = y.shape
    n = w.shape[1]
    return pl.pallas_call(
        _residual_matmul_body,
        grid=(m // ROW_TILE, n // COL_TILE),
        in_specs=[
            pl.BlockSpec((ROW_TILE, k), lambda i, j: (i, 0)),
            pl.BlockSpec((k, COL_TILE), lambda i, j: (0, j)),
            pl.BlockSpec((ROW_TILE, COL_TILE), lambda i, j: (i, j)),
        ],
        out_specs=pl.BlockSpec((ROW_TILE, COL_TILE), lambda i, j: (i, j)),
        out_shape=jax.ShapeDtypeStruct((m, n), f32),
        compiler_params=pltpu.CompilerParams(
            dimension_semantics=("arbitrary", "arbitrary"), vmem_limit_bytes=VMEM_LIMIT_BYTES),
        name="residual_matmul",
    )(y, w, x)


def masked_softmax(s, mask):
    s = jnp.where(mask, s, -1e30)
    m = jnp.max(s, axis=-1, keepdims=True)
    e = jnp.exp(s - m) * mask
    return e / jnp.maximum(jnp.sum(e, axis=-1, keepdims=True), 1e-30)


def alibi_slopes():
    h = jnp.arange(1, N_HEADS_NSA + 1, dtype=jnp.float32)
    return (2.0 ** (-8.0 * h / N_HEADS_NSA)).reshape(N_KV_NSA, HPG)


def split_projections(z):
    B, T, _ = z.shape
    sizes = [D_NSA] + [D_KV] * 6 + [3 * N_HEADS_NSA, N_HEADS_RET * RET_DK, N_HEADS_RET * RET_DK, D_RET, D_RET, D_MODEL, D_MODEL]
    parts = jnp.split(z, [int(c) for c in np.cumsum(sizes)[:-1]], axis=-1)
    q = parts[0].reshape(B, T, N_KV_NSA, HPG, HEAD_DIM)
    kvs = [p.reshape(B, T, N_KV_NSA, HEAD_DIM) for p in parts[1:7]]
    return (q, *kvs, *parts[7:])


def compress_blocks(k, pos_emb, w1, w2):
    B, L, G, D = k.shape
    lp = -(-L // CMP_STRIDE) * CMP_STRIDE
    sub = jnp.pad(k, ((0, 0), (0, lp - L), (0, 0), (0, 0))).reshape(B, lp // CMP_STRIDE, CMP_STRIDE, G, D)
    r = CMP_LEN // CMP_STRIDE
    nc = lp // CMP_STRIDE - r + 1
    blk = jnp.concatenate([sub[:, j:j + nc] for j in range(r)], axis=2)
    blk = blk + pos_emb[None, None, :, None, :]
    blk = jnp.transpose(blk, (0, 1, 3, 2, 4)).reshape(B, nc, G, CMP_LEN * D)
    return jax.nn.gelu(blk @ w1) @ w2


def nsa_compressed_selected(qg, q_pos, k_cmp, v_cmp, k_sel, v_sel, cmp_pos_k, cmp_pos_v, w_cmp_k1, w_cmp_k2, w_cmp_v1, w_cmp_v2):
    B, T = qg.shape[:2]
    L = k_cmp.shape[1]
    slopes = alibi_slopes()
    kc = compress_blocks(k_cmp, cmp_pos_k, w_cmp_k1, w_cmp_k2)
    vc = compress_blocks(v_cmp, cmp_pos_v, w_cmp_v1, w_cmp_v2)
    nc = kc.shape[1]
    c_start = jnp.arange(nc) * CMP_STRIDE
    dist_c = q_pos[:, None] - (c_start + CMP_LEN - 1)[None, :]
    s = jnp.einsum('btgrd,bngd->bgrtn', qg, kc).astype(f32) * SCALE - slopes[None, :, :, None, None] * dist_c.astype(f32)
    p_cmp = masked_softmax(s, dist_c >= 0)
    o_cmp = jnp.einsum('bgrtn,bngd->btgrd', p_cmp.astype(vc.dtype), vc)
    ns = -(-L // SEL_BLOCK)
    s_start = jnp.arange(ns) * SEL_BLOCK
    overlap = ((c_start[:, None] < s_start[None, :] + SEL_BLOCK) & (c_start[:, None] + CMP_LEN > s_start[None, :])).astype(f32)
    imp = jnp.einsum('bgrtn,ns->bgts', p_cmp, overlap)
    t_blk = q_pos // SEL_BLOCK
    blk = jnp.arange(ns)[None, :]
    forced = (blk == 0) | (blk == t_blk[:, None]) | (blk == t_blk[:, None] - 1)
    score = jnp.where(forced, 1e9, imp)
    score = jnp.where(s_start[None, :] <= q_pos[:, None], score, -1e9)
    n_top = min(SEL_TOP, ns)
    _, idx = lax.top_k(score, n_top)

    def to_blocks(a):
        a = jnp.pad(a, ((0, 0), (0, ns * SEL_BLOCK - L), (0, 0), (0, 0)))
        return a.reshape(B, ns, SEL_BLOCK, N_KV_NSA, HEAD_DIM).transpose(0, 3, 1, 2, 4)
    ks, vs = to_blocks(k_sel), to_blocks(v_sel)
    tb = Q_BLOCK if T % Q_BLOCK == 0 else T
    nb = T // tb
    q_blocks = jnp.moveaxis(qg.reshape(B, nb, tb, N_KV_NSA, HPG, HEAD_DIM), 1, 0)
    i_blocks = jnp.moveaxis(idx.reshape(B, N_KV_NSA, nb, tb, n_top), 2, 0)
    p_blocks = q_pos.reshape(nb, tb)
    gather = jax.vmap(jax.vmap(lambda kb, ix: kb[ix]))
    nk = n_top * SEL_BLOCK

    def one_block(args):
        qb, ib, pb = args
        kg = gather(ks, ib).reshape(B, N_KV_NSA, tb, nk, HEAD_DIM)
        vg = gather(vs, ib).reshape(B, N_KV_NSA, tb, nk, HEAD_DIM)
        kpos = (ib[..., None] * SEL_BLOCK + jnp.arange(SEL_BLOCK)).reshape(B, N_KV_NSA, tb, nk)
        dist = pb[None, None, :, None] - kpos
        sb = jnp.einsum('btgrd,bgtkd->bgrtk', qb, kg).astype(f32) * SCALE - slopes[None, :, :, None, None] * dist[:, :, None].astype(f32)
        pb_ = masked_softmax(sb, (dist >= 0)[:, :, None])
        return jnp.einsum('bgrtk,bgtkd->btgrd', pb_.astype(vg.dtype), vg)

    o_sel = lax.map(one_block, (q_blocks, i_blocks, p_blocks))
    o_sel = jnp.moveaxis(o_sel, 0, 1).reshape(B, T, N_KV_NSA, HPG, HEAD_DIM)
    return o_cmp, o_sel


def window_banded(qg, kw, vw):
    B, T = qg.shape[:2]
    nb, nw = T // Q_BLOCK, WINDOW // Q_BLOCK

    def bands(a):
        ap = jnp.pad(a, ((0, 0), (WINDOW, 0), (0, 0), (0, 0))).reshape(B, nb + nw, Q_BLOCK, N_KV_NSA, HEAD_DIM)
        return jnp.concatenate([ap[:, j:j + nb] for j in range(nw + 1)], axis=2)

    kb, vb = bands(kw), bands(vw)
    qb = qg.reshape(B, nb, Q_BLOCK, N_KV_NSA, HPG, HEAD_DIM)
    qpos = jnp.arange(T).reshape(nb, Q_BLOCK)
    kpos = jnp.arange(nb)[:, None] * Q_BLOCK - WINDOW + jnp.arange(WINDOW + Q_BLOCK)[None, :]
    dist = qpos[:, :, None] - kpos[:, None, :]
    mask = (dist >= 0) & (dist <= WINDOW) & (kpos[:, None, :] >= 0)
    slopes = alibi_slopes()
    s = jnp.einsum('bnqgrd,bnkgd->bngrqk', qb, kb).astype(f32) * SCALE - slopes[None, None, :, :, None, None] * dist[None, :, None, None].astype(f32)
    p = masked_softmax(s, mask[None, :, None, None])
    o = jnp.einsum('bngrqk,bnkgd->bnqgrd', p.astype(vb.dtype), vb)
    return o.reshape(B, T, N_KV_NSA, HPG, HEAD_DIM)


def window_buffer(qg, pos, kw, vw, buf_k, buf_v):
    wb, T = buf_k.shape[1], kw.shape[1]
    kk = jnp.concatenate([buf_k, kw], axis=1)
    vv = jnp.concatenate([buf_v, vw], axis=1)
    kpos = PAST_LEN - wb + jnp.arange(wb + T)
    dist = pos[:, None] - kpos[None, :]
    mask = (dist >= 0) & (dist <= WINDOW)
    slopes = alibi_slopes()
    s = jnp.einsum('btgrd,bkgd->bgrtk', qg, kk).astype(f32) * SCALE - slopes[None, :, :, None, None] * dist.astype(f32)
    p = masked_softmax(s, mask)
    o = jnp.einsum('bgrtk,bkgd->btgrd', p.astype(vv.dtype), vv)
    keep = min(WINDOW, wb + T)
    return o, kk[:, wb + T - keep:], vv[:, wb + T - keep:]


def nsa_combine(o_cmp, o_sel, o_win, gate_nsa):
    B, T = gate_nsa.shape[:2]
    g = jax.nn.sigmoid(gate_nsa).reshape(B, T, 3, N_KV_NSA, HPG, 1)
    o = g[:, :, 0] * o_cmp + g[:, :, 1] * o_sel + g[:, :, 2] * o_win
    return o.reshape(B, T, D_NSA)


def rotary(x, pos):
    half = x.shape[-1] // 2
    inv = ROPE_BASE ** (-jnp.arange(half, dtype=jnp.float32) / half)
    ang = pos.astype(jnp.float32)[:, None] * inv[None, :]
    c, s = jnp.cos(ang)[None, :, None, :], jnp.sin(ang)[None, :, None, :]
    x1, x2 = x[..., :half], x[..., half:]
    return jnp.concatenate([x1 * c - x2 * s, x1 * s + x2 * c], axis=-1)


def retention(qr, kr, vr, gr, pos, s0, chunk, ret_gn):
    B, T, _ = qr.shape
    q = rotary(qr.reshape(B, T, N_HEADS_RET, RET_DK).astype(f32), pos)
    k = rotary(kr.reshape(B, T, N_HEADS_RET, RET_DK).astype(f32), pos) * (RET_DK ** -0.5)
    v = vr.reshape(B, T, N_HEADS_RET, RET_DV).astype(f32)
    lg = jnp.log1p(-(2.0 ** (-5.0 - jnp.arange(N_HEADS_RET, dtype=f32))))
    i = jnp.arange(chunk)
    diff = i[:, None] - i[None, :]
    intra = jnp.where(diff >= 0, jnp.exp(lg[:, None, None] * jnp.maximum(diff, 0)), 0.0)
    q_dec = jnp.exp(lg[None, :] * (i[:, None] + 1))
    k_dec = jnp.exp(lg[None, :] * (chunk - 1 - i)[:, None])
    c_dec = jnp.exp(lg * chunk)
    nc = T // chunk

    def to_chunks(a):
        return jnp.moveaxis(a.reshape(B, nc, chunk, N_HEADS_RET, a.shape[-1]), 1, 0)

    def step(S, inp):
        qc, kc, vc = inp
        a = jnp.einsum('bihd,bjhd->bhij', qc, kc) * intra[None]
        o = jnp.einsum('bhij,bjhe->bihe', a, vc) + jnp.einsum('bihd,bhde->bihe', qc, S) * q_dec[None, :, :, None]
        S = S * c_dec[None, :, None, None] + jnp.einsum('bjhd,jh,bjhe->bhde', kc, k_dec, vc)
        return S, o

    S, o = lax.scan(step, s0.astype(f32), (to_chunks(q), to_chunks(k), to_chunks(v)))
    o = jnp.moveaxis(o, 0, 1).reshape(B, T, N_HEADS_RET, RET_DV)
    mu = jnp.mean(o, axis=-1, keepdims=True)
    var = jnp.mean(jnp.square(o - mu), axis=-1, keepdims=True)
    on = ((o - mu) * lax.rsqrt(var + EPS)).reshape(B, T, D_RET) * ret_gn.astype(f32)
    return jax.nn.silu(gr) * on.astype(gr.dtype), S


def gather_pages(pool, page_table):
    g = pool[page_table]
    return g.reshape(g.shape[0], g.shape[1] * g.shape[2], g.shape[3], g.shape[4])


def _prompt_mixers(z, W):
    (cmp_pos_k, cmp_pos_v, w_cmp_k1, w_cmp_k2, w_cmp_v1, w_cmp_v2, ret_gn) = W
    B, T, _ = z.shape
    pos = jnp.arange(T)
    q, kc, vc, ks, vs, kw, vw, gn, qr, kr, vr, gr, ga, gb = split_projections(z)
    o_cmp, o_sel = nsa_compressed_selected(q, pos, kc, vc, ks, vs, cmp_pos_k, cmp_pos_v, w_cmp_k1, w_cmp_k2, w_cmp_v1, w_cmp_v2)
    o_win = window_banded(q, kw, vw)
    o_nsa = nsa_combine(o_cmp, o_sel, o_win, gn)
    s0 = jnp.zeros((B, N_HEADS_RET, RET_DK, RET_DV), jnp.float32)
    o_ret, S = retention(qr, kr, vr, gr, pos, s0, RET_CHUNK if T % RET_CHUNK == 0 else T, ret_gn)
    wl = min(WINDOW, T)
    return o_nsa, o_ret, ga, gb, (kc, vc, ks, vs, kw[:, T - wl:], vw[:, T - wl:], S)


def _sample_mixers(z, c_cmp_k, c_cmp_v, c_sel_k, c_sel_v, c_win_k, c_win_v, s_ret, page_table, W):
    (cmp_pos_k, cmp_pos_v, w_cmp_k1, w_cmp_k2, w_cmp_v1, w_cmp_v2, ret_gn) = W
    B, T, _ = z.shape
    pos = PAST_LEN + jnp.arange(T)
    q, kc, vc, ks, vs, kw, vw, gn, qr, kr, vr, gr, ga, gb = split_projections(z)
    kc_all = jnp.concatenate([gather_pages(c_cmp_k, page_table), kc], axis=1)
    vc_all = jnp.concatenate([gather_pages(c_cmp_v, page_table), vc], axis=1)
    ks_all = jnp.concatenate([gather_pages(c_sel_k, page_table), ks], axis=1)
    vs_all = jnp.concatenate([gather_pages(c_sel_v, page_table), vs], axis=1)
    o_cmp, o_sel = nsa_compressed_selected(q, pos, kc_all, vc_all, ks_all, vs_all, cmp_pos_k, cmp_pos_v, w_cmp_k1, w_cmp_k2, w_cmp_v1, w_cmp_v2)
    o_win, kw_buf, vw_buf = window_buffer(q, pos, kw, vw, c_win_k, c_win_v)
    o_nsa = nsa_combine(o_cmp, o_sel, o_win, gn)
    o_ret, S = retention(qr, kr, vr, gr, pos, s_ret, T, ret_gn)
    return o_nsa, o_ret, ga, gb, (kc, vc, ks, vs, kw_buf, vw_buf, S)


def kernel(x_prompt, x_sample, cache_cmp_k, cache_cmp_v, cache_sel_k, cache_sel_v, cache_win_k, cache_win_v, state_ret, page_table, norm_ffn1, w_ffn1_gate, w_ffn1_up, w_ffn1_down, norm_mix, w_in, cmp_pos_k, cmp_pos_v, w_cmp_k1, w_cmp_k2, w_cmp_v1, w_cmp_v2, ret_gn, w_br_nsa, w_br_ret, w_out, norm_ffn2, w_ffn2_gate, w_ffn2_up, w_ffn2_down, norm_final):
    assert DEPTH == 1 and norm_ffn1.shape[0] == 1
    bp, tp, d = x_prompt.shape
    bs, ts, _ = x_sample.shape
    mp, ms = bp * tp, bs * ts
    x = jnp.concatenate([x_prompt.reshape(mp, d), x_sample.reshape(ms, d)], axis=0)

    d_in = w_in.shape[2]
    d_in_pad = -(-d_in // COL_TILE) * COL_TILE
    w_in_b = jnp.pad(w_in[0].astype(bf16), ((0, 0), (0, d_in_pad - d_in)))

    x = _ffn(x, norm_ffn1[0], w_ffn1_gate[0].astype(bf16), w_ffn1_up[0].astype(bf16), w_ffn1_down[0].astype(bf16))
    z = _norm_matmul(x, norm_mix[0], w_in_b)[:, :d_in]

    W = (cmp_pos_k[0], cmp_pos_v[0], w_cmp_k1[0], w_cmp_k2[0], w_cmp_v1[0], w_cmp_v2[0], ret_gn[0])
    on_p, or_p, ga_p, gb_p, st_p = _prompt_mixers(z[:mp].reshape(bp, tp, d_in), W)
    on_s, or_s, ga_s, gb_s, st_s = _sample_mixers(
        z[mp:].reshape(bs, ts, d_in), cache_cmp_k[0], cache_cmp_v[0], cache_sel_k[0], cache_sel_v[0],
        cache_win_k[0], cache_win_v[0], state_ret[0], page_table, W)

    def cat(a, b):
        return jnp.concatenate([a.reshape(mp, -1), b.reshape(ms, -1)], axis=0)

    y = _gated_pair_matmul(cat(on_p, on_s).astype(bf16), cat(or_p, or_s).astype(bf16), cat(ga_p, ga_s), cat(gb_p, gb_s),
                           w_br_nsa[0].astype(bf16), w_br_ret[0].astype(bf16))
    x = _residual_matmul(y, w_out[0].astype(bf16), x)
    x = _ffn(x, norm_ffn2[0], w_ffn2_gate[0].astype(bf16), w_ffn2_up[0].astype(bf16), w_ffn2_down[0].astype(bf16), norm_final)

    y_prompt = x[:mp].reshape(bp, tp, d)
    y_sample = x[mp:].reshape(bs, ts, d)
    return (y_prompt, y_sample, *[a[None] for a in st_p], *[a[None] for a in st_s])
```
